```python
import math
import jax, jax.numpy as jnp
from jax import lax
import numpy as np

D_MODEL = 1024
BATCH = 8
SEQ = 8192
DEPTH = 1

D_MIX = D_MODEL
D_SSM = D_MIX // 2
D_CONV = D_MIX - D_SSM
SSM_GROUP = 16
N_SSM_GROUPS = D_SSM // SSM_GROUP
SSM_STATE = 64
CONV_HEAD_DIM = 64
N_CONV_HEADS = D_CONV // CONV_HEAD_DIM
CONV_WIDTH = 3
D_IN_PROJ = D_SSM + 3 * D_CONV
D_FF = 4 * D_MODEL
RMS_EPS = 1e-6
DT_MIN = 1e-3
DT_MAX = 1e-1

kernel_name = "hymba_s5_shortconv_sandwich_block"


def rms_norm(x, g):
    xf = x.astype(jnp.float32)
    y = xf * lax.rsqrt(jnp.mean(xf * xf, axis=-1, keepdims=True) + RMS_EPS)
    return (y * g.astype(jnp.float32)).astype(x.dtype)


def _scan_combine(e1, e2):
    a1r, a1i, b1r, b1i = e1
    a2r, a2i, b2r, b2i = e2
    ar = a2r * a1r - a2i * a1i
    ai = a2r * a1i + a2i * a1r
    br = a2r * b1r - a2i * b1i + b2r
    bi = a2r * b1i + a2i * b1r + b2i
    return (ar, ai, br, bi)


def s5_group_mixer(u, lam_re, lam_im, log_dt, b_re, b_im, c_re, c_im, d_skip, w_glu):
    bsz, seq, _ = u.shape
    uf = u.astype(jnp.float32).reshape(bsz, seq, N_SSM_GROUPS, SSM_GROUP)
    lr = lam_re.astype(jnp.float32)
    li = lam_im.astype(jnp.float32)
    dt = jnp.exp(log_dt.astype(jnp.float32))[:, None]
    mag = jnp.exp(lr * dt)
    abr = mag * jnp.cos(li * dt)
    abi = mag * jnp.sin(li * dt)
    nr, ni = abr - 1.0, abi
    den = lr * lr + li * li
    coef_r = (nr * lr + ni * li) / den
    coef_i = (ni * lr - nr * li) / den
    br_ = b_re.astype(jnp.float32)
    bi_ = b_im.astype(jnp.float32)
    bbar_r = coef_r[..., None] * br_ - coef_i[..., None] * bi_
    bbar_i = coef_r[..., None] * bi_ + coef_i[..., None] * br_
    bu_r = jnp.einsum('blgh,gph->blgp', uf, bbar_r)
    bu_i = jnp.einsum('blgh,gph->blgp', uf, bbar_i)
    a_r = jnp.broadcast_to(abr, bu_r.shape)
    a_i = jnp.broadcast_to(abi, bu_i.shape)
    _, _, xr, xi = lax.associative_scan(_scan_combine, (a_r, a_i, bu_r, bu_i), axis=1)
    y = (jnp.einsum('blgp,ghp->blgh', xr, c_re.astype(jnp.float32))
         - jnp.einsum('blgp,ghp->blgh', xi, c_im.astype(jnp.float32)))
    y = y + d_skip.astype(jnp.float32) * uf
    y = jax.nn.gelu(y.reshape(bsz, seq, D_SSM))
    y = y * jax.nn.sigmoid(y @ w_glu.astype(jnp.float32))
    return y.astype(u.dtype)


def short_conv_mixer(h, b_gate, c_gate, conv_w):
    z = c_gate * h
    zp = jnp.pad(z, ((0, 0), (CONV_WIDTH - 1, 0), (0, 0)))
    conv = (conv_w[0] * zp[:, :-2] + conv_w[1] * zp[:, 1:-1] + conv_w[2] * zp[:, 2:])
    return b_gate * conv


def setup_inputs(seed: int = 0) -> dict:
    key = jax.random.key(seed)
    ks = jax.random.split(key, 24)
    f32 = jnp.float32
    L = DEPTH
    x = jax.random.normal(ks[0], (BATCH, SEQ, D_MODEL), f32)

    def gain(k, n):
        return 1.0 + 0.01 * jax.random.normal(k, (L, n), f32)

    n_idx = jnp.arange(SSM_STATE, dtype=f32)
    lam_re = -0.5 + 0.01 * jax.random.normal(ks[1], (L, N_SSM_GROUPS, SSM_STATE), f32)
    lam_im = math.pi * n_idx + 0.01 * jax.random.normal(ks[2], (L, N_SSM_GROUPS, SSM_STATE), f32)
    log_dt = jax.random.uniform(ks[3], (L, N_SSM_GROUPS), f32, math.log(DT_MIN), math.log(DT_MAX))
    b_scale = (2.0 * SSM_GROUP) ** -0.5
    c_scale = (2.0 * SSM_STATE) ** -0.5
    return {
        "x": x,
        "g_pre_mix": gain(ks[4], D_MODEL),
        "w_in": jax.random.normal(ks[5], (L, D_MODEL, D_IN_PROJ), f32) * D_MODEL ** -0.5,
        "lam_re": lam_re,
        "lam_im": lam_im,
        "log_dt": log_dt,
        "b_re": jax.random.normal(ks[6], (L, N_SSM_GROUPS, SSM_STATE, SSM_GROUP), f32) * b_scale,
        "b_im": jax.random.normal(ks[7], (L, N_SSM_GROUPS, SSM_STATE, SSM_GROUP), f32) * b_scale,
        "c_re": jax.random.normal(ks[8], (L, N_SSM_GROUPS, SSM_GROUP, SSM_STATE), f32) * c_scale,
        "c_im": jax.random.normal(ks[9], (L, N_SSM_GROUPS, SSM_GROUP, SSM_STATE), f32) * c_scale,
        "d_skip": jax.random.normal(ks[10], (L, N_SSM_GROUPS, SSM_GROUP), f32),
        "w_glu": jax.random.normal(ks[11], (L, D_SSM, D_SSM), f32) * D_SSM ** -0.5,
        "conv_w": jax.random.normal(ks[12], (L, CONV_WIDTH, D_CONV), f32) * CONV_WIDTH ** -0.5,
        "g_ssm_out": gain(ks[13], D_SSM),
        "g_conv_out": gain(ks[14], D_CONV),
        "w_out": jax.random.normal(ks[15], (L, D_MIX, D_MODEL), f32) * D_MIX ** -0.5,
        "g_post_mix": gain(ks[16], D_MODEL),
        "g_pre_mlp": gain(ks[17], D_MODEL),
        "w_up": jax.random.normal(ks[18], (L, D_MODEL, D_FF), f32) * D_MODEL ** -0.5,
        "w_down": jax.random.normal(ks[19], (L, D_FF, D_MODEL), f32) * D_FF ** -0.5,
        "g_post_mlp": gain(ks[20], D_MODEL),
    }


def reference(x, g_pre_mix, w_in, lam_re, lam_im, log_dt, b_re, b_im, c_re, c_im, d_skip,
              w_glu, conv_w, g_ssm_out, g_conv_out, w_out, g_post_mix, g_pre_mlp, w_up,
              w_down, g_post_mlp):
    for i in range(DEPTH):
        hn = rms_norm(x, g_pre_mix[i])
        proj = hn @ w_in[i]
        u_ssm = proj[..., :D_SSM]
        h_conv = proj[..., D_SSM:D_SSM + D_CONV]
        b_gate = proj[..., D_SSM + D_CONV:D_SSM + 2 * D_CONV]
        c_gate = proj[..., D_SSM + 2 * D_CONV:]
        y_ssm = s5_group_mixer(u_ssm, lam_re[i], lam_im[i], log_dt[i], b_re[i], b_im[i],
                               c_re[i], c_im[i], d_skip[i], w_glu[i])
        y_conv = short_conv_mixer(h_conv, b_gate, c_gate, conv_w[i])
        y = jnp.concatenate([rms_norm(y_ssm, g_ssm_out[i]),
                             rms_norm(y_conv, g_conv_out[i])], axis=-1)
        x = x + rms_norm(y @ w_out[i], g_post_mix[i])
        hn = rms_norm(x, g_pre_mlp[i])
        m = jnp.square(jax.nn.relu(hn @ w_up[i])) @ w_down[i]
        x = x + rms_norm(m, g_post_mlp[i])
    return x
```

```python
import functools

import jax
import jax.numpy as jnp
from jax import lax
from jax.experimental import pallas as pl
from jax.experimental.pallas import tpu as pltpu

RMS_EPS = 1e-6
SSM_GROUP = 16
SSM_STATE = 64
CONV_WIDTH = 3
HALO = 8
LANES = 128
VMEM_LIMIT_BYTES = 56 * 1024 * 1024

F32 = jnp.float32
BF16 = jnp.bfloat16


def _rms(x, g):
    return x * lax.rsqrt(jnp.mean(x * x, axis=-1, keepdims=True) + RMS_EPS) * g


def _const_spec(shape):
    nd = len(shape)
    return pl.BlockSpec(shape, lambda *_: (0,) * nd, pipeline_mode=pl.Buffered(1))


def _zoh_kernel(lr_ref, li_ref, ldt_ref, brt_ref, bit_ref,
                ar_ref, ai_ref, bbr_ref, bbi_ref):
    lr = lr_ref[...]
    li = li_ref[...]
    dt = jnp.exp(ldt_ref[...])
    mag = jnp.exp(lr * dt)
    abr = mag * jnp.cos(li * dt)
    abi = mag * jnp.sin(li * dt)
    nr, ni = abr - 1.0, abi
    den = lr * lr + li * li
    cr = ((nr * lr + ni * li) / den)[:, None, :]
    ci = ((ni * lr - nr * li) / den)[:, None, :]
    ar_ref[...] = abr
    ai_ref[...] = abi
    brt = brt_ref[...]
    bit = bit_ref[...]
    bbr_ref[...] = cr * brt - ci * bit
    bbi_ref[...] = cr * bit + ci * brt


def _zoh(lam_re, lam_im, log_dt, b_re, b_im):
    g, p = lam_re.shape
    h = b_re.shape[-1]
    brt = jnp.swapaxes(b_re, 1, 2)
    bit = jnp.swapaxes(b_im, 1, 2)
    return pl.pallas_call(
        _zoh_kernel,
        out_shape=(jax.ShapeDtypeStruct((g, p), F32), jax.ShapeDtypeStruct((g, p), F32),
                   jax.ShapeDtypeStruct((g, h, p), F32), jax.ShapeDtypeStruct((g, h, p), F32)),
        name="s5_zoh",
    )(lam_re, lam_im, log_dt.reshape(g, 1), brt, bit)


def _inproj_kernel(x_ref, g_ref, win_ref, cw_ref, gc_ref, u_ref, yc_ref, z_ref,
                   *, nb, tt, d_ssm, d_conv):
    ti = pl.program_id(0)

    @pl.when(ti == 0)
    def _():
        z_ref[:, 0:HALO, :] = jnp.zeros((nb, HALO, d_conv), F32)

    x = x_ref[...].reshape(nb * tt, x_ref.shape[-1])
    hn = _rms(x, g_ref[...]).astype(BF16)
    proj = jnp.dot(hn, win_ref[...], preferred_element_type=F32)

    for b in range(nb):
        for j in range(d_ssm // LANES):
            u_ref[j, pl.ds(b, tt, stride=nb), :] = proj[b * tt:(b + 1) * tt, j * LANES:(j + 1) * LANES]

    h = proj[:, d_ssm:d_ssm + d_conv]
    bg = proj[:, d_ssm + d_conv:d_ssm + 2 * d_conv]
    cg = proj[:, d_ssm + 2 * d_conv:]
    z = cg * h
    z_ref[:, HALO:HALO + tt, :] = z.reshape(nb, tt, d_conv)
    z1 = z_ref[:, HALO - 1:HALO - 1 + tt, :].reshape(nb * tt, d_conv)
    z2 = z_ref[:, HALO - 2:HALO - 2 + tt, :].reshape(nb * tt, d_conv)
    z_ref[:, 0:HALO, :] = z_ref[:, tt:tt + HALO, :]
    cw = cw_ref[...]
    conv = cw[0:1, :] * z2 + cw[1:2, :] * z1 + cw[2:3, :] * z
    yc = _rms(bg * conv, gc_ref[...])
    yc_ref[...] = yc.reshape(nb, tt, d_conv).astype(yc_ref.dtype)


def _scan_kernel(u_ref, bbd_ref, cbd_ref, ar_ref, ai_ref, dsk_ref, wglu_ref, gs_ref,
                 o_ref, bu_ref, hst_ref, y_ref, *, nb, ts, n_state, col_chunk):
    ti = pl.program_id(0)

    @pl.when(ti == 0)
    def _():
        hst_ref[...] = jnp.zeros_like(hst_ref)

    n_tiles = u_ref.shape[0]
    u = jnp.concatenate([u_ref[j] for j in range(n_tiles)], axis=-1)
    bu_ref[...] = jnp.dot(u.astype(BF16), bbd_ref[...], preferred_element_type=F32)

    for c in range(n_state // col_chunk):
        re = pl.ds(c * col_chunk, col_chunk)
        im = pl.ds(n_state + c * col_chunk, col_chunk)
        ar = jnp.broadcast_to(ar_ref[:, re], (nb, col_chunk))
        ai = jnp.broadcast_to(ai_ref[:, re], (nb, col_chunk))

        def body(t, carry, re=re, im=im, ar=ar, ai=ai):
            hr, hi = carry
            rows = pl.ds(pl.multiple_of(t * nb, nb), nb)
            nr = ar * hr - ai * hi + bu_ref[rows, re]
            ni = ar * hi + ai * hr + bu_ref[rows, im]
            bu_ref[rows, re] = nr
            bu_ref[rows, im] = ni
            return nr, ni

        hr, hi = lax.fori_loop(0, ts, body, (hst_ref[:, re], hst_ref[:, im]), unroll=8)
        hst_ref[:, re] = hr
        hst_ref[:, im] = hi

    y = jnp.dot(bu_ref[...].astype(BF16), cbd_ref[...], preferred_element_type=F32)
    y = y + dsk_ref[...] * u
    y = jax.nn.gelu(y)
    y = y * jax.nn.sigmoid(jnp.dot(y.astype(BF16), wglu_ref[...], preferred_element_type=F32))
    y = _rms(y, gs_ref[...])
    for j in range(n_tiles):
        y_ref[j] = y[:, j * LANES:(j + 1) * LANES]
    for b in range(nb):
        o_ref[b] = jnp.concatenate(
            [y_ref[j, pl.ds(b, ts, stride=nb), :] for j in range(n_tiles)], axis=-1
        ).astype(o_ref.dtype)


def _mlp_kernel(x_ref, ys_ref, yc_ref, wos_ref, woc_ref, gpm_ref, gpre_ref,
                wup_ref, wdn_ref, gpost_ref, o_ref):
    m = (jnp.dot(ys_ref[...], wos_ref[...], preferred_element_type=F32)
         + jnp.dot(yc_ref[...], woc_ref[...], preferred_element_type=F32))
    x1 = x_ref[...] + _rms(m, gpm_ref[...])
    hn = _rms(x1, gpre_ref[...]).astype(BF16)
    hid = jnp.dot(hn, wup_ref[...], preferred_element_type=F32)
    hid = jnp.square(jnp.maximum(hid, 0.0)).astype(BF16)
    m2 = jnp.dot(hid, wdn_ref[...], preferred_element_type=F32)
    o_ref[...] = x1 + _rms(m2, gpost_ref[...])


def _block(x, g_pre_mix, w_in, lam_re, lam_im, log_dt, b_re, b_im, c_re, c_im, d_skip,
           w_glu, conv_w, g_ssm_out, g_conv_out, w_out, g_post_mix, g_pre_mlp, w_up,
           w_down, g_post_mlp, *, tt, ts, tm):
    nb, seq, d = x.shape
    n_groups, n_p = lam_re.shape
    d_ssm = n_groups * SSM_GROUP
    d_in = w_in.shape[1]
    d_conv = (d_in - d_ssm) // 3
    d_ff = w_up.shape[1]
    n_state = n_groups * n_p
    n_tiles = d_ssm // LANES
    assert seq % tt == 0 and seq % ts == 0 and (nb * seq) % tm == 0
    params = pltpu.CompilerParams(dimension_semantics=("arbitrary",),
                                  vmem_limit_bytes=VMEM_LIMIT_BYTES)

    a_r, a_i, bbr, bbi = _zoh(lam_re, lam_im, log_dt, b_re, b_im)
    eye = jnp.eye(n_groups, dtype=F32)
    bbd = jnp.concatenate(
        [jnp.einsum('ghp,gk->ghkp', bbr, eye).reshape(d_ssm, n_state),
         jnp.einsum('ghp,gk->ghkp', bbi, eye).reshape(d_ssm, n_state)], axis=1).astype(BF16)
    cbd = jnp.concatenate(
        [jnp.einsum('ghp,gk->gpkh', c_re, eye).reshape(n_state, d_ssm),
         jnp.einsum('ghp,gk->gpkh', -c_im, eye).reshape(n_state, d_ssm)], axis=0).astype(BF16)

    row = lambda v: v.reshape(1, -1)

    u_tm, y_conv = pl.pallas_call(
        functools.partial(_inproj_kernel, nb=nb, tt=tt, d_ssm=d_ssm, d_conv=d_conv),
        grid=(seq // tt,),
        in_specs=[pl.BlockSpec((nb, tt, d), lambda i: (0, i, 0)),
                  _const_spec((1, d)), _const_spec((d, d_in)),
                  _const_spec((CONV_WIDTH, d_conv)), _const_spec((1, d_conv))],
        out_specs=(pl.BlockSpec((n_tiles, tt * nb, LANES), lambda i: (0, i, 0)),
                   pl.BlockSpec((nb, tt, d_conv), lambda i: (0, i, 0))),
        out_shape=(jax.ShapeDtypeStruct((n_tiles, seq * nb, LANES), F32),
                   jax.ShapeDtypeStruct((nb, seq, d_conv), BF16)),
        scratch_shapes=[pltpu.VMEM((nb, tt + HALO, d_conv), F32)],
        compiler_params=params,
        name="inproj_conv",
    )(x, row(g_pre_mix), w_in.astype(BF16), conv_w, row(g_conv_out))

    y_ssm = pl.pallas_call(
        functools.partial(_scan_kernel, nb=nb, ts=ts, n_state=n_state, col_chunk=512),
        grid=(seq // ts,),
        in_specs=[pl.BlockSpec((n_tiles, ts * nb, LANES), lambda i: (0, i, 0)),
                  _const_spec((d_ssm, 2 * n_state)), _const_spec((2 * n_state, d_ssm)),
                  _const_spec((1, n_state)), _const_spec((1, n_state)),
                  _const_spec((1, d_ssm)), _const_spec((d_ssm, d_ssm)), _const_spec((1, d_ssm))],
        out_specs=pl.BlockSpec((nb, ts, d_ssm), lambda i: (0, i, 0)),
        out_shape=jax.ShapeDtypeStruct((nb, seq, d_ssm), BF16),
        scratch_shapes=[pltpu.VMEM((ts * nb, 2 * n_state), F32),
                        pltpu.VMEM((nb, 2 * n_state), F32),
                        pltpu.VMEM((n_tiles, ts * nb, LANES), F32)],
        compiler_params=params,
        name="s5_scan",
    )(u_tm, bbd, cbd, row(a_r), row(a_i), row(d_skip), w_glu.astype(BF16), row(g_ssm_out))

    n_tok = nb * seq
    w_out_b = w_out.astype(BF16)
    out = pl.pallas_call(
        _mlp_kernel,
        grid=(n_tok // tm,),
        in_specs=[pl.BlockSpec((tm, d), lambda i: (i, 0)),
                  pl.BlockSpec((tm, d_ssm), lambda i: (i, 0)),
                  pl.BlockSpec((tm, d_conv), lambda i: (i, 0)),
                  _const_spec((d_ssm, d)), _const_spec((d_conv, d)),
                  _const_spec((1, d)), _const_spec((1, d)),
                  _const_spec((d, d_ff)), _const_spec((d_ff, d)), _const_spec((1, d))],
        out_specs=pl.BlockSpec((tm, d), lambda i: (i, 0)),
        out_shape=jax.ShapeDtypeStruct((n_tok, d), F32),
        compiler_params=params,
        name="outproj_mlp",
    )(x.reshape(n_tok, d), y_ssm.reshape(n_tok, d_ssm), y_conv.reshape(n_tok, d_conv),
      w_out_b[:d_ssm], w_out_b[d_ssm:], row(g_post_mix), row(g_pre_mlp),
      w_up.astype(BF16), w_down.astype(BF16), row(g_post_mlp))
    return out.reshape(nb, seq, d)


def kernel(x, g_pre_mix, w_in, lam_re, lam_im, log_dt, b_re, b_im, c_re, c_im, d_skip, w_glu, conv_w, g_ssm_out, g_conv_out, w_out, g_post_mix, g_pre_mlp, w_up, w_down, g_post_mlp):
    depth = w_in.shape[0]
    for i in range(depth):
        x = _block(x, g_pre_mix[i], w_in[i], lam_re[i], lam_im[i], log_dt[i], b_re[i], b_im[i],
                   c_re[i], c_im[i], d_skip[i], w_glu[i], conv_w[i], g_ssm_out[i], g_conv_out[i],
                   w_out[i], g_post_mix[i], g_pre_mlp[i], w_up[i], w_down[i], g_post_mlp[i],
                   tt=64, ts=64, tm=512)
    return x
```

```python
import functools

import jax
import jax.numpy as jnp
from jax import lax
from jax.experimental import pallas as pl
from jax.experimental.pallas import tpu as pltpu

RMS_EPS = 1e-6
SSM_GROUP = 16
CONV_WIDTH = 3
HALO = 8
LANES = 128
CHUNK = 4
VMEM_LIMIT_BYTES = 56 * 1024 * 1024

F32 = jnp.float32
BF16 = jnp.bfloat16
HIGHEST = lax.Precision.HIGHEST


def _rms(x, g):
    return x * lax.rsqrt(jnp.mean(x * x, axis=-1, keepdims=True) + RMS_EPS) * g


def _const_spec(shape):
    nd = len(shape)
    return pl.BlockSpec(shape, lambda *_: (0,) * nd, pipeline_mode=pl.Buffered(1))


def _zoh_kernel(lr_ref, li_ref, ldt_ref, brt_ref, bit_ref, cr_ref, ci_ref,
                atr_ref, ati_ref, wbr_ref, wbi_ref, wcr_ref, wci_ref, kt_ref, *, chunk):
    lr = lr_ref[...]
    li = li_ref[...]
    dt = jnp.exp(ldt_ref[...])
    mag = jnp.exp(lr * dt)
    abr = mag * jnp.cos(li * dt)
    abi = mag * jnp.sin(li * dt)
    nr, ni = abr - 1.0, abi
    den = lr * lr + li * li
    cfr = ((nr * lr + ni * li) / den)[:, None, :]
    cfi = ((ni * lr - nr * li) / den)[:, None, :]
    brt = brt_ref[...]
    bit = bit_ref[...]
    bbr = cfr * brt - cfi * bit
    bbi = cfr * bit + cfi * brt

    pr = [jnp.ones_like(abr)[:, None, :], abr[:, None, :]]
    pi = [jnp.zeros_like(abi)[:, None, :], abi[:, None, :]]
    for _ in range(2, chunk + 1):
        pr.append(pr[-1] * pr[1] - pi[-1] * pi[1])
        pi.append(pr[-2] * pi[1] + pi[-1] * pr[1])
    atr_ref[...] = pr[chunk][:, 0, :]
    ati_ref[...] = pi[chunk][:, 0, :]

    for i in range(chunk):
        n = chunk - 1 - i
        wbr_ref[i] = pr[n] * bbr - pi[n] * bbi
        wbi_ref[i] = pr[n] * bbi + pi[n] * bbr

    c_re = cr_ref[...]
    c_im = ci_ref[...]
    cnr = [c_re * pr[n] - c_im * pi[n] for n in range(chunk + 1)]
    cni = [c_re * pi[n] + c_im * pr[n] for n in range(chunk + 1)]
    for i in range(chunk):
        wcr_ref[i] = cnr[i + 1]
        wci_ref[i] = -cni[i + 1]
    ein = functools.partial(jnp.einsum, 'ghp,gkp->ghk', precision=HIGHEST,
                            preferred_element_type=F32)
    for n in range(chunk):
        kt_ref[n] = ein(cnr[n], bbr) - ein(cni[n], bbi)


def _zoh(lam_re, lam_im, log_dt, b_re, b_im, c_re, c_im, chunk):
    g, p = lam_re.shape
    h = b_re.shape[-1]
    brt = jnp.swapaxes(b_re, 1, 2)
    bit = jnp.swapaxes(b_im, 1, 2)
    gp = jax.ShapeDtypeStruct((g, p), F32)
    tghp = jax.ShapeDtypeStruct((chunk, g, h, p), F32)
    return pl.pallas_call(
        functools.partial(_zoh_kernel, chunk=chunk),
        out_shape=(gp, gp, tghp, tghp, tghp, tghp,
                   jax.ShapeDtypeStruct((chunk, g, h, h), F32)),
        name="s5_zoh",
    )(lam_re, lam_im, log_dt.reshape(g, 1), brt, bit, c_re, c_im)


def _pack_scan_weights(wbr, wbi, wcr, wci, kt, n_tiles):
    chunk, n_groups, h, p = wbr.shape
    gl = n_groups // n_tiles
    eye = jnp.eye(gl, dtype=F32)
    t5 = lambda w: w.reshape(chunk, n_tiles, gl, w.shape[2], w.shape[3])
    wb = jnp.concatenate(
        [jnp.einsum('ijghp,gk->jighkp', t5(w), eye).reshape(n_tiles, chunk * gl * h, gl * p)
         for w in (wbr, wbi)], axis=-1)
    wcs = jnp.concatenate(
        [jnp.einsum('ijghp,gk->jkpigh', t5(w), eye).reshape(n_tiles, gl * p, chunk * gl * h)
         for w in (wcr, wci)], axis=1)
    lag = jnp.arange(chunk)
    sel = (lag[None, None, :] - lag[None, :, None] == lag[:, None, None]).astype(F32)
    wct = jnp.einsum('njghk,nab,gl->jalkbgh', t5(kt), sel, eye).reshape(
        n_tiles, chunk * gl * h, chunk * gl * h)
    wc = jnp.concatenate([wcs, wct], axis=1)
    return wb.astype(BF16), wc.astype(BF16)


def _inproj_kernel(x_ref, g_ref, win_ref, cw_ref, gc_ref, u_ref, yc_ref, z_ref,
                   *, nb, tt, d_ssm, d_conv):
    ti = pl.program_id(0)

    @pl.when(ti == 0)
    def _():
        z_ref[:, 0:HALO, :] = jnp.zeros((nb, HALO, d_conv), F32)

    x = x_ref[...].reshape(nb * tt, x_ref.shape[-1])
    hn = _rms(x, g_ref[...]).astype(BF16)
    proj = jnp.dot(hn, win_ref[...], preferred_element_type=F32)

    for b in range(nb):
        for j in range(d_ssm // LANES):
            u_ref[j, pl.ds(b, tt, stride=nb), :] = proj[b * tt:(b + 1) * tt, j * LANES:(j + 1) * LANES]

    h = proj[:, d_ssm:d_ssm + d_conv]
    bg = proj[:, d_ssm + d_conv:d_ssm + 2 * d_conv]
    cg = proj[:, d_ssm + 2 * d_conv:]
    z = cg * h
    z_ref[:, HALO:HALO + tt, :] = z.reshape(nb, tt, d_conv)
    z1 = z_ref[:, HALO - 1:HALO - 1 + tt, :].reshape(nb * tt, d_conv)
    z2 = z_ref[:, HALO - 2:HALO - 2 + tt, :].reshape(nb * tt, d_conv)
    z_ref[:, 0:HALO, :] = z_ref[:, tt:tt + HALO, :]
    cw = cw_ref[...]
    conv = cw[0:1, :] * z2 + cw[1:2, :] * z1 + cw[2:3, :] * z
    yc = _rms(bg * conv, gc_ref[...])
    yc_ref[...] = yc.reshape(nb, tt, d_conv).astype(yc_ref.dtype)


def _scan_kernel(u_ref, wb_ref, wc_ref, atr_ref, ati_ref, dsk_ref, wglu_ref, gs_ref,
                 o_ref, s_ref, hst_ref, yc_ref, yt_ref, *, nb, ts, chunk):
    ti = pl.program_id(0)

    @pl.when(ti == 0)
    def _():
        hst_ref[...] = jnp.zeros_like(hst_ref)

    n_tiles = u_ref.shape[0]
    nc = ts // chunk
    rows = nc * nb
    sw = s_ref.shape[1] // n_tiles
    half = sw // 2

    def chunk_rows(j):
        return jnp.concatenate(
            [u_ref[j, :, i].reshape(rows, LANES) for i in range(chunk)], axis=-1).astype(BF16)

    for j in range(n_tiles):
        s_ref[:, j * sw:(j + 1) * sw] = jnp.dot(chunk_rows(j), wb_ref[j], preferred_element_type=F32)

    for j in range(n_tiles):
        re = pl.ds(j * sw, half)
        im = pl.ds(j * sw + half, half)
        ar = jnp.broadcast_to(atr_ref[:, j * half:(j + 1) * half], (nb, half))
        ai = jnp.broadcast_to(ati_ref[:, j * half:(j + 1) * half], (nb, half))

        def body(k, carry, re=re, im=im, ar=ar, ai=ai):
            hr, hi = carry
            r = pl.ds(pl.multiple_of(k * nb, nb), nb)
            vr = s_ref[r, re]
            vi = s_ref[r, im]
            s_ref[r, re] = hr
            s_ref[r, im] = hi
            return ar * hr - ai * hi + vr, ar * hi + ai * hr + vi

        hr, hi = lax.fori_loop(0, nc, body, (hst_ref[:, re], hst_ref[:, im]), unroll=8)
        hst_ref[:, re] = hr
        hst_ref[:, im] = hi

    for j in range(n_tiles):
        lhs = jnp.concatenate([s_ref[:, j * sw:(j + 1) * sw].astype(BF16), chunk_rows(j)], axis=-1)
        yj = jnp.dot(lhs, wc_ref[j], preferred_element_type=F32)
        dj = dsk_ref[:, j * LANES:(j + 1) * LANES]
        for i in range(chunk):
            yi = yj[:, i * LANES:(i + 1) * LANES] + dj * u_ref[j, :, i].reshape(rows, LANES)
            yc_ref[j, :, i] = yi.reshape(nc, nb, LANES)

    y = jnp.concatenate([yc_ref[j].reshape(ts * nb, LANES) for j in range(n_tiles)], axis=-1)
    y = jax.nn.gelu(y)
    y = y * jax.nn.sigmoid(jnp.dot(y.astype(BF16), wglu_ref[...], preferred_element_type=F32))
    y = _rms(y, gs_ref[...])
    for j in range(n_tiles):
        yt_ref[j] = y[:, j * LANES:(j + 1) * LANES]
    for b in range(nb):
        o_ref[b] = jnp.concatenate(
            [yt_ref[j, pl.ds(b, ts, stride=nb), :] for j in range(n_tiles)], axis=-1
        ).astype(o_ref.dtype)


def _mlp_kernel(x_ref, ys_ref, yc_ref, wos_ref, woc_ref, gpm_ref, gpre_ref,
                wup_ref, wdn_ref, gpost_ref, o_ref):
    m = (jnp.dot(ys_ref[...], wos_ref[...], preferred_element_type=F32)
         + jnp.dot(yc_ref[...], woc_ref[...], preferred_element_type=F32))
    x1 = x_ref[...] + _rms(m, gpm_ref[...])
    hn = _rms(x1, gpre_ref[...]).astype(BF16)
    hid = jnp.dot(hn, wup_ref[...], preferred_element_type=F32)
    hid = jnp.square(jnp.maximum(hid, 0.0)).astype(BF16)
    m2 = jnp.dot(hid, wdn_ref[...], preferred_element_type=F32)
    o_ref[...] = x1 + _rms(m2, gpost_ref[...])


def _block(x, g_pre_mix, w_in, lam_re, lam_im, log_dt, b_re, b_im, c_re, c_im, d_skip,
           w_glu, conv_w, g_ssm_out, g_conv_out, w_out, g_post_mix, g_pre_mlp, w_up,
           w_down, g_post_mlp, *, tt, ts, tm):
    nb, seq, d = x.shape
    n_groups, n_p = lam_re.shape
    d_ssm = n_groups * SSM_GROUP
    d_in = w_in.shape[1]
    d_conv = (d_in - d_ssm) // 3
    d_ff = w_up.shape[1]
    n_state = n_groups * n_p
    n_tiles = d_ssm // LANES
    assert seq % tt == 0 and seq % ts == 0 and ts % CHUNK == 0 and (nb * seq) % tm == 0
    params = pltpu.CompilerParams(dimension_semantics=("arbitrary",),
                                  vmem_limit_bytes=VMEM_LIMIT_BYTES)

    at_r, at_i, wbr, wbi, wcr, wci, kt = _zoh(lam_re, lam_im, log_dt, b_re, b_im, c_re, c_im, CHUNK)
    wb, wc = _pack_scan_weights(wbr, wbi, wcr, wci, kt, n_tiles)

    row = lambda v: v.reshape(1, -1)

    u_tm, y_conv = pl.pallas_call(
        functools.partial(_inproj_kernel, nb=nb, tt=tt, d_ssm=d_ssm, d_conv=d_conv),
        grid=(seq // tt,),
        in_specs=[pl.BlockSpec((nb, tt, d), lambda i: (0, i, 0)),
                  _const_spec((1, d)), _const_spec((d, d_in)),
                  _const_spec((CONV_WIDTH, d_conv)), _const_spec((1, d_conv))],
        out_specs=(pl.BlockSpec((n_tiles, tt * nb, LANES), lambda i: (0, i, 0)),
                   pl.BlockSpec((nb, tt, d_conv), lambda i: (0, i, 0))),
        out_shape=(jax.ShapeDtypeStruct((n_tiles, seq * nb, LANES), F32),
                   jax.ShapeDtypeStruct((nb, seq, d_conv), BF16)),
        scratch_shapes=[pltpu.VMEM((nb, tt + HALO, d_conv), F32)],
        compiler_params=params,
        name="inproj_conv",
    )(x, row(g_pre_mix), w_in.astype(BF16), conv_w, row(g_conv_out))

    nc = ts // CHUNK
    u5 = u_tm.reshape(n_tiles, seq // CHUNK, CHUNK, nb, LANES)
    y_ssm = pl.pallas_call(
        functools.partial(_scan_kernel, nb=nb, ts=ts, chunk=CHUNK),
        grid=(seq // ts,),
        in_specs=[pl.BlockSpec((n_tiles, nc, CHUNK, nb, LANES), lambda i: (0, i, 0, 0, 0)),
                  _const_spec(wb.shape), _const_spec(wc.shape),
                  _const_spec((1, n_state)), _const_spec((1, n_state)),
                  _const_spec((1, d_ssm)), _const_spec((d_ssm, d_ssm)), _const_spec((1, d_ssm))],
        out_specs=pl.BlockSpec((nb, ts, d_ssm), lambda i: (0, i, 0)),
        out_shape=jax.ShapeDtypeStruct((nb, seq, d_ssm), BF16),
        scratch_shapes=[pltpu.VMEM((nc * nb, 2 * n_state), F32),
                        pltpu.VMEM((nb, 2 * n_state), F32),
                        pltpu.VMEM((n_tiles, nc, CHUNK, nb, LANES), F32),
                        pltpu.VMEM((n_tiles, ts * nb, LANES), F32)],
        compiler_params=params,
        name="s5_scan",
    )(u5, wb, wc, row(at_r), row(at_i), row(d_skip), w_glu.astype(BF16), row(g_ssm_out))

    n_tok = nb * seq
    w_out_b = w_out.astype(BF16)
    out = pl.pallas_call(
        _mlp_kernel,
        grid=(n_tok // tm,),
        in_specs=[pl.BlockSpec((tm, d), lambda i: (i, 0)),
                  pl.BlockSpec((tm, d_ssm), lambda i: (i, 0)),
                  pl.BlockSpec((tm, d_conv), lambda i: (i, 0)),
                  _const_spec((d_ssm, d)), _const_spec((d_conv, d)),
                  _const_spec((1, d)), _const_spec((1, d)),
                  _const_spec((d, d_ff)), _const_spec((d_ff, d)), _const_spec((1, d))],
        out_specs=pl.BlockSpec((tm, d), lambda i: (i, 0)),
        out_shape=jax.ShapeDtypeStruct((n_tok, d), F32),
        compiler_params=params,
        name="outproj_mlp",
    )(x.reshape(n_tok, d), y_ssm.reshape(n_tok, d_ssm), y_conv.reshape(n_tok, d_conv),
      w_out_b[:d_ssm], w_out_b[d_ssm:], row(g_post_mix), row(g_pre_mlp),
      w_up.astype(BF16), w_down.astype(BF16), row(g_post_mlp))
    return out.reshape(nb, seq, d)


def kernel(x, g_pre_mix, w_in, lam_re, lam_im, log_dt, b_re, b_im, c_re, c_im, d_skip, w_glu, conv_w, g_ssm_out, g_conv_out, w_out, g_post_mix, g_pre_mlp, w_up, w_down, g_post_mlp):
    depth = w_in.shape[0]
    for i in range(depth):
        x = _block(x, g_pre_mix[i], w_in[i], lam_re[i], lam_im[i], log_dt[i], b_re[i], b_im[i],
                   c_re[i], c_im[i], d_skip[i], w_glu[i], conv_w[i], g_ssm_out[i], g_conv_out[i],
                   w_out[i], g_post_mix[i], g_pre_mlp[i], w_up[i], w_down[i], g_post_mlp[i],
                   tt=128, ts=128, tm=512)
    return x
```

```python
import functools

import jax
import jax.numpy as jnp
import numpy as np
from jax import lax
from jax.experimental import pallas as pl
from jax.experimental.pallas import tpu as pltpu

RMS_EPS = 1e-6
SSM_GROUP = 16
CONV_WIDTH = 3
HALO = 8
LANES = 128
CHUNK = 4
VMEM_LIMIT_BYTES = 56 * 1024 * 1024

F32 = jnp.float32
BF16 = jnp.bfloat16
HIGHEST = lax.Precision.HIGHEST


def _rms(x, g):
    return x * lax.rsqrt(jnp.mean(x * x, axis=-1, keepdims=True) + RMS_EPS) * g


def _const_spec(shape):
    nd = len(shape)
    return pl.BlockSpec(shape, lambda *_: (0,) * nd, pipeline_mode=pl.Buffered(1))


def _zoh_kernel(lr_ref, li_ref, ldt_ref, brt_ref, bit_ref, cr_ref, ci_ref,
                atr_ref, ati_ref, wbr_ref, wbi_ref, wcr_ref, wci_ref, kt_ref, *, chunk):
    lr = lr_ref[...]
    li = li_ref[...]
    dt = jnp.exp(ldt_ref[...])
    mag = jnp.exp(lr * dt)
    abr = mag * jnp.cos(li * dt)
    abi = mag * jnp.sin(li * dt)
    nr, ni = abr - 1.0, abi
    den = lr * lr + li * li
    cfr = ((nr * lr + ni * li) / den)[:, None, :]
    cfi = ((ni * lr - nr * li) / den)[:, None, :]
    brt = brt_ref[...]
    bit = bit_ref[...]
    bbr = cfr * brt - cfi * bit
    bbi = cfr * bit + cfi * brt

    pr = [jnp.ones_like(abr)[:, None, :], abr[:, None, :]]
    pi = [jnp.zeros_like(abi)[:, None, :], abi[:, None, :]]
    for _ in range(2, chunk + 1):
        pr.append(pr[-1] * pr[1] - pi[-1] * pi[1])
        pi.append(pr[-2] * pi[1] + pi[-1] * pr[1])
    atr_ref[...] = pr[chunk][:, 0, :]
    ati_ref[...] = pi[chunk][:, 0, :]

    for i in range(chunk):
        n = chunk - 1 - i
        wbr_ref[i] = pr[n] * bbr - pi[n] * bbi
        wbi_ref[i] = pr[n] * bbi + pi[n] * bbr

    c_re = cr_ref[...]
    c_im = ci_ref[...]
    cnr = [c_re * pr[n] - c_im * pi[n] for n in range(chunk + 1)]
    cni = [c_re * pi[n] + c_im * pr[n] for n in range(chunk + 1)]
    for i in range(chunk):
        wcr_ref[i] = cnr[i + 1]
        wci_ref[i] = -cni[i + 1]
    ein = functools.partial(jnp.einsum, 'ghp,gkp->ghk', precision=HIGHEST,
                            preferred_element_type=F32)
    for n in range(chunk):
        kt_ref[n] = ein(cnr[n], bbr) - ein(cni[n], bbi)


def _zoh(lam_re, lam_im, log_dt, b_re, b_im, c_re, c_im, chunk):
    g, p = lam_re.shape
    h = b_re.shape[-1]
    brt = jnp.swapaxes(b_re, 1, 2)
    bit = jnp.swapaxes(b_im, 1, 2)
    gp = jax.ShapeDtypeStruct((g, p), F32)
    tghp = jax.ShapeDtypeStruct((chunk, g, h, p), F32)
    return pl.pallas_call(
        functools.partial(_zoh_kernel, chunk=chunk),
        out_shape=(gp, gp, tghp, tghp, tghp, tghp,
                   jax.ShapeDtypeStruct((chunk, g, h, h), F32)),
        name="s5_zoh",
    )(lam_re, lam_im, log_dt.reshape(g, 1), brt, bit, c_re, c_im)


def _pack_scan_weights(wbr, wbi, wcr, wci, kt, n_tiles):
    chunk, n_groups, h, p = wbr.shape
    gl = n_groups // n_tiles
    n_st = 2 * gl * p
    n_ch = chunk * gl * h
    same_g_state = np.zeros((gl, 2, gl, p), bool)
    same_g_chunk = np.zeros((gl, chunk, gl, h), bool)
    for g in range(gl):
        same_g_state[g, :, g, :] = True
        same_g_chunk[g, :, g, :] = True
    same_g_state = same_g_state.reshape(1, 1, gl, 1, n_st)
    same_g_chunk = same_g_chunk.reshape(1, 1, gl, 1, n_ch)
    t6 = lambda wr, wi: jnp.stack([wr, wi]).reshape(2, chunk, n_tiles, gl, h, p)

    small = t6(wbr, wbi).transpose(2, 1, 4, 0, 3, 5).reshape(n_tiles, chunk, 1, h, n_st)
    wb = jnp.where(same_g_state, small, 0.0).reshape(n_tiles, n_ch, n_st)
    small = t6(wcr, wci).transpose(2, 0, 5, 1, 3, 4).reshape(n_tiles, 2, 1, p, n_ch)
    wcs = jnp.where(same_g_chunk, small, 0.0).reshape(n_tiles, n_st, n_ch)
    kt5 = kt.reshape(chunk, n_tiles, gl, h, h)
    zero = jnp.zeros_like(kt5[0])
    lagged = jnp.stack([jnp.stack([kt5[i - ip] if i >= ip else zero for i in range(chunk)])
                        for ip in range(chunk)])
    small = lagged.transpose(2, 0, 5, 1, 3, 4).reshape(n_tiles, chunk, 1, h, n_ch)
    wct = jnp.where(same_g_chunk, small, 0.0).reshape(n_tiles, n_ch, n_ch)
    wc = jnp.concatenate([wcs, wct], axis=1)
    return wb.astype(BF16), wc.astype(BF16)


def _inproj_kernel(x_ref, g_ref, win_ref, cw_ref, gc_ref, u_ref, yc_ref, z_ref,
                   *, nb, bsub, tt, d_ssm, d_conv):
    ti = pl.program_id(0)

    @pl.when(ti == 0)
    def _():
        z_ref[:, 0:HALO, :] = jnp.zeros((nb, HALO, d_conv), F32)

    cw = cw_ref[...]

    def project(b0):
        x = x_ref[b0:b0 + bsub].reshape(bsub * tt, x_ref.shape[-1])
        hn = _rms(x, g_ref[...]).astype(BF16)
        return jnp.dot(hn, win_ref[...], preferred_element_type=F32)

    def finish(b0, proj):
        bs = slice(b0, b0 + bsub)
        for b in range(bsub):
            for j in range(d_ssm // LANES):
                u_ref[j, pl.ds(b0 + b, tt, stride=nb), :] = (
                    proj[b * tt:(b + 1) * tt, j * LANES:(j + 1) * LANES])

        h = proj[:, d_ssm:d_ssm + d_conv]
        bg = proj[:, d_ssm + d_conv:d_ssm + 2 * d_conv]
        cg = proj[:, d_ssm + 2 * d_conv:]
        z = cg * h
        z_ref[bs, HALO:HALO + tt, :] = z.reshape(bsub, tt, d_conv)
        z1 = z_ref[bs, HALO - 1:HALO - 1 + tt, :].reshape(bsub * tt, d_conv)
        z2 = z_ref[bs, HALO - 2:HALO - 2 + tt, :].reshape(bsub * tt, d_conv)
        z_ref[bs, 0:HALO, :] = z_ref[bs, tt:tt + HALO, :]
        conv = cw[0:1, :] * z2 + cw[1:2, :] * z1 + cw[2:3, :] * z
        yc = _rms(bg * conv, gc_ref[...])
        yc_ref[bs] = yc.reshape(bsub, tt, d_conv).astype(yc_ref.dtype)

    starts = list(range(0, nb, bsub))
    proj = project(starts[0])
    for b0, b1 in zip(starts, starts[1:] + [None]):
        nxt = project(b1) if b1 is not None else None
        finish(b0, proj)
        proj = nxt


def _scan_kernel(u_ref, wb_ref, wc_ref, atr_ref, ati_ref, dsk_ref, wglu_ref, gs_ref,
                 o_ref, s_ref, hst_ref, yc_ref, yt_ref, *, nb, ts, chunk):
    ti = pl.program_id(0)

    @pl.when(ti == 0)
    def _():
        hst_ref[...] = jnp.zeros_like(hst_ref)

    n_tiles = u_ref.shape[0]
    nc = ts // chunk
    rows = nc * nb
    sw = s_ref.shape[1] // n_tiles
    half = sw // 2

    def chunk_rows(j):
        return jnp.concatenate(
            [u_ref[j, :, i].reshape(rows, LANES) for i in range(chunk)], axis=-1).astype(BF16)

    for j in range(n_tiles):
        s_ref[:, j * sw:(j + 1) * sw] = jnp.dot(chunk_rows(j), wb_ref[j], preferred_element_type=F32)

    for j in range(n_tiles):
        re = pl.ds(j * sw, half)
        im = pl.ds(j * sw + half, half)
        ar = jnp.broadcast_to(atr_ref[:, j * half:(j + 1) * half], (nb, half))
        ai = jnp.broadcast_to(ati_ref[:, j * half:(j + 1) * half], (nb, half))

        def body(k, carry, re=re, im=im, ar=ar, ai=ai):
            hr, hi = carry
            r = pl.ds(pl.multiple_of(k * nb, nb), nb)
            vr = s_ref[r, re]
            vi = s_ref[r, im]
            s_ref[r, re] = hr
            s_ref[r, im] = hi
            return ar * hr - ai * hi + vr, ar * hi + ai * hr + vi

        hr, hi = lax.fori_loop(0, nc, body, (hst_ref[:, re], hst_ref[:, im]), unroll=8)
        hst_ref[:, re] = hr
        hst_ref[:, im] = hi

    for j in range(n_tiles):
        lhs = jnp.concatenate([s_ref[:, j * sw:(j + 1) * sw].astype(BF16), chunk_rows(j)], axis=-1)
        yj = jnp.dot(lhs, wc_ref[j], preferred_element_type=F32)
        dj = dsk_ref[:, j * LANES:(j + 1) * LANES]
        for i in range(chunk):
            yi = yj[:, i * LANES:(i + 1) * LANES] + dj * u_ref[j, :, i].reshape(rows, LANES)
            yc_ref[j, :, i] = yi.reshape(nc, nb, LANES)

    y = jnp.concatenate([yc_ref[j].reshape(ts * nb, LANES) for j in range(n_tiles)], axis=-1)
    y = jax.nn.gelu(y)
    y = y * jax.nn.sigmoid(jnp.dot(y.astype(BF16), wglu_ref[...], preferred_element_type=F32))
    y = _rms(y, gs_ref[...])
    for j in range(n_tiles):
        yt_ref[j] = y[:, j * LANES:(j + 1) * LANES]
    for b in range(nb):
        o_ref[b] = jnp.concatenate(
            [yt_ref[j, pl.ds(b, ts, stride=nb), :] for j in range(n_tiles)], axis=-1
        ).astype(o_ref.dtype)


def _mlp_kernel(x_ref, ys_ref, yc_ref, wos_ref, woc_ref, gpm_ref, gpre_ref,
                wup_ref, wdn_ref, gpost_ref, o_ref, *, sub):
    n = x_ref.shape[0] // sub
    rows = [pl.ds(r * sub, sub) for r in range(n)]
    x1, hid = {}, {}
    for s in range(n + 2):
        if s < n:
            m = (jnp.dot(ys_ref[rows[s], :], wos_ref[...], preferred_element_type=F32)
                 + jnp.dot(yc_ref[rows[s], :], woc_ref[...], preferred_element_type=F32))
            x1[s] = x_ref[rows[s], :] + _rms(m, gpm_ref[...])
        if 0 <= s - 1 < n:
            hn = _rms(x1[s - 1], gpre_ref[...]).astype(BF16)
            h = jnp.dot(hn, wup_ref[...], preferred_element_type=F32)
            hid[s - 1] = jnp.square(jnp.maximum(h, 0.0)).astype(BF16)
        if 0 <= s - 2 < n:
            m2 = jnp.dot(hid.pop(s - 2), wdn_ref[...], preferred_element_type=F32)
            o_ref[rows[s - 2], :] = x1.pop(s - 2) + _rms(m2, gpost_ref[...])


def _block(x, g_pre_mix, w_in, lam_re, lam_im, log_dt, b_re, b_im, c_re, c_im, d_skip,
           w_glu, conv_w, g_ssm_out, g_conv_out, w_out, g_post_mix, g_pre_mlp, w_up,
           w_down, g_post_mlp, *, tt, ts, tm):
    nb, seq, d = x.shape
    n_groups, n_p = lam_re.shape
    d_ssm = n_groups * SSM_GROUP
    d_in = w_in.shape[1]
    d_conv = (d_in - d_ssm) // 3
    d_ff = w_up.shape[1]
    n_state = n_groups * n_p
    n_tiles = d_ssm // LANES
    assert seq % tt == 0 and seq % ts == 0 and ts % CHUNK == 0 and (nb * seq) % tm == 0
    params = pltpu.CompilerParams(dimension_semantics=("arbitrary",),
                                  vmem_limit_bytes=VMEM_LIMIT_BYTES)

    at_r, at_i, wbr, wbi, wcr, wci, kt = _zoh(lam_re, lam_im, log_dt, b_re, b_im, c_re, c_im, CHUNK)
    wb, wc = _pack_scan_weights(wbr, wbi, wcr, wci, kt, n_tiles)

    row = lambda v: v.reshape(1, -1)

    u_tm, y_conv = pl.pallas_call(
        functools.partial(_inproj_kernel, nb=nb, bsub=2, tt=tt, d_ssm=d_ssm, d_conv=d_conv),
        grid=(seq // tt,),
        in_specs=[pl.BlockSpec((nb, tt, d), lambda i: (0, i, 0)),
                  _const_spec((1, d)), _const_spec((d, d_in)),
                  _const_spec((CONV_WIDTH, d_conv)), _const_spec((1, d_conv))],
        out_specs=(pl.BlockSpec((n_tiles, tt * nb, LANES), lambda i: (0, i, 0)),
                   pl.BlockSpec((nb, tt, d_conv), lambda i: (0, i, 0))),
        out_shape=(jax.ShapeDtypeStruct((n_tiles, seq * nb, LANES), F32),
                   jax.ShapeDtypeStruct((nb, seq, d_conv), BF16)),
        scratch_shapes=[pltpu.VMEM((nb, tt + HALO, d_conv), F32)],
        compiler_params=params,
        name="inproj_conv",
    )(x, row(g_pre_mix), w_in.astype(BF16), conv_w, row(g_conv_out))

    nc = ts // CHUNK
    u5 = u_tm.reshape(n_tiles, seq // CHUNK, CHUNK, nb, LANES)
    y_ssm = pl.pallas_call(
        functools.partial(_scan_kernel, nb=nb, ts=ts, chunk=CHUNK),
        grid=(seq // ts,),
        in_specs=[pl.BlockSpec((n_tiles, nc, CHUNK, nb, LANES), lambda i: (0, i, 0, 0, 0)),
                  _const_spec(wb.shape), _const_spec(wc.shape),
                  _const_spec((1, n_state)), _const_spec((1, n_state)),
                  _const_spec((1, d_ssm)), _const_spec((d_ssm, d_ssm)), _const_spec((1, d_ssm))],
        out_specs=pl.BlockSpec((nb, ts, d_ssm), lambda i: (0, i, 0)),
        out_shape=jax.ShapeDtypeStruct((nb, seq, d_ssm), BF16),
        scratch_shapes=[pltpu.VMEM((nc * nb, 2 * n_state), F32),
                        pltpu.VMEM((nb, 2 * n_state), F32),
                        pltpu.VMEM((n_tiles, nc, CHUNK, nb, LANES), F32),
                        pltpu.VMEM((n_tiles, ts * nb, LANES), F32)],
        compiler_params=params,
        name="s5_scan",
    )(u5, wb, wc, row(at_r), row(at_i), row(d_skip), w_glu.astype(BF16), row(g_ssm_out))

    n_tok = nb * seq
    w_out_b = w_out.astype(BF16)
    out = pl.pallas_call(
        functools.partial(_mlp_kernel, sub=256),
        grid=(n_tok // tm,),
        in_specs=[pl.BlockSpec((tm, d), lambda i: (i, 0)),
                  pl.BlockSpec((tm, d_ssm), lambda i: (i, 0)),
                  pl.BlockSpec((tm, d_conv), lambda i: (i, 0)),
                  _const_spec((d_ssm, d)), _const_spec((d_conv, d)),
                  _const_spec((1, d)), _const_spec((1, d)),
                  _const_spec((d, d_ff)), _const_spec((d_ff, d)), _const_spec((1, d))],
        out_specs=pl.BlockSpec((tm, d), lambda i: (i, 0)),
        out_shape=jax.ShapeDtypeStruct((n_tok, d), F32),
        compiler_params=params,
        name="outproj_mlp",
    )(x.reshape(n_tok, d), y_ssm.reshape(n_tok, d_ssm), y_conv.reshape(n_tok, d_conv),
      w_out_b[:d_ssm], w_out_b[d_ssm:], row(g_post_mix), row(g_pre_mlp),
      w_up.astype(BF16), w_down.astype(BF16), row(g_post_mlp))
    return out.reshape(nb, seq, d)


def kernel(x, g_pre_mix, w_in, lam_re, lam_im, log_dt, b_re, b_im, c_re, c_im, d_skip, w_glu, conv_w, g_ssm_out, g_conv_out, w_out, g_post_mix, g_pre_mlp, w_up, w_down, g_post_mlp):
    depth = w_in.shape[0]
    for i in range(depth):
        x = _block(x, g_pre_mix[i], w_in[i], lam_re[i], lam_im[i], log_dt[i], b_re[i], b_im[i],
                   c_re[i], c_im[i], d_skip[i], w_glu[i], conv_w[i], g_ssm_out[i], g_conv_out[i],
                   w_out[i], g_post_mix[i], g_pre_mlp[i], w_up[i], w_down[i], g_post_mlp[i],
                   tt=128, ts=128, tm=1024)
    return x
```
